```python
import math
import jax
import jax.numpy as jnp
from jax import lax
import numpy as np

D_MODEL = 2048
BATCH = 32
SEQ = 256
DEPTH = 4
DEC_BATCH = 8
DEC_SEQ = 1024
PAST_LEN = 256

GRID_W = 64
EPS = 1e-6
NEG_INF = -1e30
Q_BLOCK = 128

MIX_WIDTH = D_MODEL
GROUP_WIDTH = MIX_WIDTH // 4
HEAD_DIM = 128

NA_HEADS = GROUP_WIDTH // HEAD_DIM
NA_KH = 8
NA_KW = 16
NA_KB = 2 * NA_KW
NA_NCB = GRID_W // NA_KW

GDN_HEADS = GROUP_WIDTH // HEAD_DIM
GDN_DK = HEAD_DIM
GDN_DV = HEAD_DIM
GDN_CONV = 4
GDN_CHUNK = 64
GDN_QKV = GDN_HEADS * (2 * GDN_DK + GDN_DV)

CONV_CH = GROUP_WIDTH
CONV_WIDTH = 31

MLA_HEADS = GROUP_WIDTH // HEAD_DIM
MLA_Q_RANK = 3 * D_MODEL // 16
MLA_KV_RANK = D_MODEL // 16
MLA_NOPE = HEAD_DIM
MLA_ROPE = HEAD_DIM // 2
MLA_V = HEAD_DIM
ROPE_BASE = 10000.0

N_EXPERTS = 32
TOP_K = 4
D_EXPERT = D_MODEL
SWIGLU_LIMIT = 7.0
SWIGLU_ALPHA = 1.702
MOE_BLOCK = 128

SPLIT_SIZES = (3 * NA_HEADS * HEAD_DIM, GDN_QKV + GDN_HEADS * GDN_DV, 2 * GDN_HEADS, 2 * GDN_HEADS,
               2 * CONV_CH, MLA_Q_RANK, MLA_KV_RANK, MLA_ROPE)
SPLIT_OFFSETS = tuple(int(v) for v in np.cumsum(SPLIT_SIZES)[:-1])
IN_WIDTH = int(sum(SPLIT_SIZES))

kernel_name = 'hybrid_flow_backbone_step'


def rmsnorm(x, g):
    xf = x.astype(jnp.float32)
    y = xf * lax.rsqrt(jnp.mean(xf * xf, axis=-1, keepdims=True) + EPS)
    return (y * g.astype(jnp.float32)).astype(x.dtype)


def layernorm(x, g, b):
    xf = x.astype(jnp.float32)
    mu = jnp.mean(xf, axis=-1, keepdims=True)
    xc = xf - mu
    y = xc * lax.rsqrt(jnp.mean(xc * xc, axis=-1, keepdims=True) + EPS)
    return (y * g.astype(jnp.float32) + b.astype(jnp.float32)).astype(x.dtype)


def l2norm(x):
    return x * lax.rsqrt(jnp.sum(x * x, axis=-1, keepdims=True) + EPS)


def depthwise_conv(x, w, pad_l, pad_r):
    return lax.conv_general_dilated(x, w[:, None, :].astype(x.dtype), window_strides=(1,),
                                    padding=[(pad_l, pad_r)], dimension_numbers=('NWC', 'WIO', 'NWC'),
                                    feature_group_count=x.shape[-1])


def axial_rope(x, pos_r, pos_c):
    half = x.shape[-1] // 2
    inv = 1.0 / (ROPE_BASE ** (jnp.arange(0, half, 2, dtype=jnp.float32) / half))

    def rot(xa, pos):
        ang = pos[:, None] * inv[None, :]
        ang = jnp.concatenate([ang, ang], axis=-1)
        shape = (pos.shape[0],) + (1,) * (xa.ndim - 3) + (half,)
        cos = jnp.cos(ang).reshape(shape).astype(xa.dtype)
        sin = jnp.sin(ang).reshape(shape).astype(xa.dtype)
        x1, x2 = jnp.split(xa, 2, axis=-1)
        return xa * cos + jnp.concatenate([-x2, x1], axis=-1) * sin

    return jnp.concatenate([rot(x[..., :half], pos_r), rot(x[..., half:], pos_c)], axis=-1)


def dense_attention(q, k, v):
    B, S, H, dq = q.shape
    nb = S // Q_BLOCK
    scale = dq ** -0.5
    qb = jnp.swapaxes(q.reshape(B, nb, Q_BLOCK, H, dq), 0, 1)

    def block(qi):
        s = jnp.einsum('bqhd,bkhd->bhqk', qi, k).astype(jnp.float32) * scale
        pr = jax.nn.softmax(s, axis=-1).astype(v.dtype)
        return jnp.einsum('bhqk,bkhd->bqhd', pr, v)

    o = lax.map(block, qb)
    return jnp.swapaxes(o, 0, 1).reshape(B, S, H, v.shape[-1])


def neighbourhood_attention(q, k, v, k_ctx, v_ctx, rpb):
    B, S, H, dh = q.shape
    rows = S // GRID_W
    kh = min(NA_KH, rows)
    scale = dh ** -0.5
    r = np.arange(rows)
    row_start = np.clip(r - kh // 2, 0, rows - kh)
    row_idx = row_start[:, None] + np.arange(kh)
    q_cols = np.arange(GRID_W).reshape(NA_NCB, NA_KW)
    col_start = np.clip(q_cols - NA_KW // 2, 0, GRID_W - NA_KW)
    band_start = np.minimum(col_start[:, 0], GRID_W - NA_KB)
    band_cols = band_start[:, None] + np.arange(NA_KB)
    n_keys = kh * NA_KB
    tok_idx = (row_idx[:, None, :, None] * GRID_W + band_cols[None, :, None, :]).reshape(rows, NA_NCB, n_keys)
    in_win = (band_cols[:, None, :] >= col_start[:, :, None]) & (band_cols[:, None, :] < col_start[:, :, None] + NA_KW)
    mask = np.broadcast_to(in_win[:, :, None, :], (NA_NCB, NA_KW, kh, NA_KB)).reshape(NA_NCB, NA_KW, n_keys)
    dr = row_idx - r[:, None]
    dc = band_cols[:, None, :] - q_cols[:, :, None]
    ridx = (dr + NA_KH - 1)[:, None, None, :, None]
    cidx = np.clip(dc + NA_KW - 1, 0, 2 * NA_KW - 2)[None, :, :, None, :]
    bias = rpb[:, ridx, cidx].reshape(H, rows, NA_NCB, NA_KW, n_keys).astype(jnp.float32)
    kg = jnp.take(k, tok_idx, axis=1)
    vg = jnp.take(v, tok_idx, axis=1)
    qb = q.reshape(B, rows, NA_NCB, NA_KW, H, dh)
    s_loc = jnp.einsum('brnqhd,brnkhd->bhrnqk', qb, kg).astype(jnp.float32) * scale + bias[None]
    s_loc = jnp.where(mask, s_loc, NEG_INF)
    s_ctx = jnp.einsum('brnqhd,blhd->bhrnql', qb, k_ctx).astype(jnp.float32) * scale
    pr = jax.nn.softmax(jnp.concatenate([s_loc, s_ctx], axis=-1), axis=-1).astype(v.dtype)
    o = (jnp.einsum('bhrnqk,brnkhd->brnqhd', pr[..., :n_keys], vg)
         + jnp.einsum('bhrnql,blhd->brnqhd', pr[..., n_keys:], v_ctx))
    return o.reshape(B, S, H, dh)


def chunk_gated_delta(q, k, v, g, beta, s0):
    B, T, H, dk = q.shape
    n = T // GDN_CHUNK

    def chunks(t):
        t = t.reshape((B, n, GDN_CHUNK, H) + t.shape[3:])
        return jnp.moveaxis(t, (1, 3), (0, 2))

    q = chunks(q) * dk ** -0.5
    k = chunks(k)
    v = chunks(v)
    beta = chunks(beta)
    gc = jnp.cumsum(chunks(g), axis=-1)
    idx = jnp.arange(GDN_CHUNK)
    tril = idx[:, None] >= idx[None, :]
    strict = idx[:, None] > idx[None, :]
    decay = jnp.exp(jnp.where(tril, gc[..., :, None] - gc[..., None, :], -jnp.inf))
    kb = k * beta[..., None]
    amat = jnp.where(strict, jnp.einsum('...id,...jd->...ij', kb, k) * decay, 0.0) + jnp.eye(GDN_CHUNK, dtype=jnp.float32)
    u = lax.linalg.triangular_solve(amat, v * beta[..., None], left_side=True, lower=True, unit_diagonal=True)
    w = lax.linalg.triangular_solve(amat, kb * jnp.exp(gc)[..., None], left_side=True, lower=True, unit_diagonal=True)
    qk = jnp.where(tril, jnp.einsum('...id,...jd->...ij', q, k) * decay, 0.0)

    def step(s, xs):
        q_i, k_i, u_i, w_i, gc_i, qk_i = xs
        v_new = u_i - jnp.einsum('bhcd,bhde->bhce', w_i, s)
        o = jnp.einsum('bhcd,bhde->bhce', q_i * jnp.exp(gc_i)[..., None], s) + jnp.einsum('bhij,bhje->bhie', qk_i, v_new)
        g_last = gc_i[..., -1]
        s = s * jnp.exp(g_last)[..., None, None] + jnp.einsum(
            'bhcd,bhce->bhde', k_i * jnp.exp(g_last[..., None] - gc_i)[..., None], v_new)
        return s, o

    s_fin, o = lax.scan(step, s0.astype(jnp.float32), (q, k, u, w, gc, qk))
    o = jnp.moveaxis(o, (0, 2), (1, 3)).reshape(B, T, H, v.shape[-1])
    return o, s_fin


def gdn_mixer(qkvz, beta_raw, a_raw, conv_w, a_log, dt_bias, norm_g, s0):
    B, T, _ = qkvz.shape
    qkv = jax.nn.silu(depthwise_conv(qkvz[..., :GDN_QKV], conv_w, (GDN_CONV - 1) // 2, GDN_CONV // 2)).astype(jnp.float32)
    z = qkvz[..., GDN_QKV:].reshape(B, T, GDN_HEADS, GDN_DV)
    q, k, v = jnp.split(qkv, (GDN_HEADS * GDN_DK, 2 * GDN_HEADS * GDN_DK), axis=-1)
    q = l2norm(q.reshape(B, T, GDN_HEADS, GDN_DK))
    k = l2norm(k.reshape(B, T, GDN_HEADS, GDN_DK))
    v = v.reshape(B, T, GDN_HEADS, GDN_DV)
    beta = jax.nn.sigmoid(beta_raw.astype(jnp.float32)).reshape(B, T, 2, GDN_HEADS)
    g = -jnp.exp(a_log.astype(jnp.float32)) * jax.nn.softplus(
        a_raw.astype(jnp.float32).reshape(B, T, 2, GDN_HEADS) + dt_bias.astype(jnp.float32))

    def flip(t):
        return jnp.flip(t, axis=1)

    o_f, s_f = chunk_gated_delta(q, k, v, g[:, :, 0], beta[:, :, 0], s0[:, 0])
    o_b, s_b = chunk_gated_delta(flip(q), flip(k), flip(v), flip(g[:, :, 1]), flip(beta[:, :, 1]), s0[:, 1])
    o = rmsnorm(o_f + flip(o_b), norm_g) * jax.nn.silu(z.astype(jnp.float32))
    return o.reshape(B, T, GDN_HEADS * GDN_DV).astype(qkvz.dtype), jnp.stack([s_f, s_b], axis=1)


def conformer_conv(u, dw_w, dw_b, ln_g, ln_b):
    a, b = jnp.split(u, 2, axis=-1)
    h = a * jax.nn.sigmoid(b)
    h = depthwise_conv(h, dw_w, CONV_WIDTH // 2, CONV_WIDTH // 2) + dw_b
    return jax.nn.silu(layernorm(h, ln_g, ln_b))


def mla_expand(ckv_n, w_ukv):
    B, M, _ = ckv_n.shape
    kv = (ckv_n @ w_ukv).reshape(B, M, MLA_HEADS, MLA_NOPE + MLA_V)
    return kv[..., :MLA_NOPE], kv[..., MLA_NOPE:]


def mla_keys(k_nope, k_rope):
    B, M, H, _ = k_nope.shape
    return jnp.concatenate([k_nope, jnp.broadcast_to(k_rope[:, :, None, :], (B, M, H, k_rope.shape[-1]))], axis=-1)


def token_mixers(h, p, lat):
    B, T, _ = h.shape
    a_qkv, b_qkvz, b_beta, b_a, c_in, d_cq, d_ckv, d_kr = jnp.split(h @ p['w_in'], SPLIT_OFFSETS, axis=-1)
    a_qkv = a_qkv.reshape(B, T, 3, NA_HEADS, HEAD_DIM)
    qa, ka, va = a_qkv[:, :, 0], a_qkv[:, :, 1], a_qkv[:, :, 2]
    qd = (rmsnorm(d_cq, p['mla_q_norm_g']) @ p['mla_w_uq']).reshape(B, T, MLA_HEADS, MLA_NOPE + MLA_ROPE)
    ckv_n = rmsnorm(d_ckv, p['mla_kv_norm_g'])
    k_nope, vd = mla_expand(ckv_n, p['mla_w_ukv'])
    if lat is None:
        oa = dense_attention(qa, ka, va)
        s0 = jnp.zeros((B, 2, GDN_HEADS, GDN_DK, GDN_DV), jnp.float32)
        od = dense_attention(qd, mla_keys(k_nope, d_kr), vd)
    else:
        pos_r, pos_c, ctx_k, ctx_v, s0, ctx_ckv, ctx_kr = lat
        oa = neighbourhood_attention(qa, ka, va, ctx_k, ctx_v, p['na_rpb'])
        q_rot = jnp.concatenate([qd[..., :MLA_NOPE], axial_rope(qd[..., MLA_NOPE:], pos_r, pos_c)], axis=-1)
        kc_nope, vc = mla_expand(ctx_ckv, p['mla_w_ukv'])
        keys = jnp.concatenate([mla_keys(k_nope, axial_rope(d_kr, pos_r, pos_c)), mla_keys(kc_nope, ctx_kr)], axis=1)
        od = dense_attention(q_rot, keys, jnp.concatenate([vd, vc], axis=1))
    ob, s_fin = gdn_mixer(b_qkvz, b_beta, b_a, p['gdn_conv_w'], p['gdn_a_log'], p['gdn_dt_bias'], p['gdn_norm_g'], s0)
    oc = conformer_conv(c_in, p['conv_dw_w'], p['conv_dw_b'], p['conv_ln_g'], p['conv_ln_b'])
    mix = jnp.concatenate([oa.reshape(B, T, -1), ob, oc, od.reshape(B, T, -1)], axis=-1) @ p['w_out']
    ctx_tensors = (ka, va, s_fin, ckv_n, d_kr) if lat is None else None
    return mix, ctx_tensors


def clamped_swiglu(gu):
    x_glu, x_lin = jnp.split(gu, 2, axis=-1)
    x_glu = jnp.minimum(x_glu, SWIGLU_LIMIT)
    x_lin = jnp.clip(x_lin, -SWIGLU_LIMIT, SWIGLU_LIMIT)
    return x_glu * jax.nn.sigmoid(SWIGLU_ALPHA * x_glu) * (x_lin + 1.0)


def moe(h, router_w, router_b, w_gu, b_gu, w_dn, b_dn):
    B, T, D = h.shape
    n = B * T
    x = h.reshape(n, D)
    logits = (x @ router_w + router_b).astype(jnp.float32)
    top_v, top_i = lax.top_k(logits, TOP_K)
    gates = jax.nn.softmax(top_v, axis=-1)
    flat_e = top_i.reshape(-1).astype(jnp.int32)
    flat_t = jnp.repeat(jnp.arange(n, dtype=jnp.int32), TOP_K)
    flat_g = gates.reshape(-1)
    counts = jnp.bincount(flat_e, length=N_EXPERTS).astype(jnp.int32)
    padded = (counts + MOE_BLOCK - 1) // MOE_BLOCK * MOE_BLOCK
    pad_end = jnp.cumsum(padded)
    pad_start = pad_end - padded
    start = jnp.cumsum(counts) - counts
    order = jnp.argsort(flat_e)
    e_sorted = flat_e[order]
    dest = pad_start[e_sorted] + jnp.arange(n * TOP_K, dtype=jnp.int32) - start[e_sorted]
    n_rows = ((n * TOP_K + MOE_BLOCK - 1) // MOE_BLOCK + N_EXPERTS) * MOE_BLOCK
    n_blocks = n_rows // MOE_BLOCK
    row_tok = jnp.zeros((n_rows,), jnp.int32).at[dest].set(flat_t[order])
    row_gate = jnp.zeros((n_rows,), jnp.float32).at[dest].set(flat_g[order])
    blk_exp = jnp.minimum(jnp.searchsorted(pad_end, jnp.arange(n_blocks, dtype=jnp.int32) * MOE_BLOCK, side='right'),
                          N_EXPERTS - 1)

    def expert_block(args):
        tok, e = args
        y = clamped_swiglu(x[tok] @ w_gu[e] + b_gu[e])
        return y @ w_dn[e] + b_dn[e]

    out = lax.map(expert_block, (row_tok.reshape(n_blocks, MOE_BLOCK), blk_exp))
    y = jnp.zeros((n, D), jnp.float32).at[row_tok].add(out.reshape(n_rows, D).astype(jnp.float32) * row_gate[:, None])
    return y.astype(h.dtype).reshape(B, T, D)


def trunk_layer(x, cvec, p, lat):
    sh1, sc1, g1, sh2, sc2, g2 = jnp.split((jax.nn.silu(cvec) @ p['w_ada'] + p['b_ada'])[:, None, :], 6, axis=-1)
    h = rmsnorm(x, p['ln1_g']) * (1.0 + sc1) + sh1
    mix, ctx_tensors = token_mixers(h, p, lat)
    x = x + g1 * mix
    h = rmsnorm(x, p['ln2_g']) * (1.0 + sc2) + sh2
    x = x + g2 * moe(h, p['router_w'], p['router_b'], p['exp_w_gu'], p['exp_b_gu'], p['exp_w_down'], p['exp_b_down'])
    return x, ctx_tensors


def setup_inputs(seed: int = 0) -> dict:
    key = jax.random.key(seed)
    ks = list(jax.random.split(key, 40))

    def nrm(shape, scale):
        return jax.random.normal(ks.pop(), shape, jnp.float32) * scale

    def gain(shape):
        return 1.0 + nrm(shape, 0.02)

    dt = jnp.exp(jax.random.uniform(ks.pop(), (DEPTH, 2, GDN_HEADS), jnp.float32, math.log(1e-3), math.log(1e-1)))
    a_log = jnp.log(jax.random.uniform(ks.pop(), (DEPTH, 2, GDN_HEADS), jnp.float32, 1.0, 16.0))
    return {
        'x_prompt': nrm((BATCH, SEQ, D_MODEL), 1.0),
        'x_sample': nrm((DEC_BATCH, DEC_SEQ, D_MODEL), 1.0),
        'c': nrm((DEC_BATCH, D_MODEL), 1.0),
        'cache_na_k': nrm((DEC_BATCH, DEPTH, PAST_LEN, NA_HEADS, HEAD_DIM), 1.0),
        'cache_na_v': nrm((DEC_BATCH, DEPTH, PAST_LEN, NA_HEADS, HEAD_DIM), 1.0),
        'state_gdn': nrm((DEC_BATCH, DEPTH, 2, GDN_HEADS, GDN_DK, GDN_DV), 0.1),
        'cache_mla_ckv': nrm((DEC_BATCH, DEPTH, PAST_LEN, MLA_KV_RANK), 1.0),
        'cache_mla_krope': nrm((DEC_BATCH, DEPTH, PAST_LEN, MLA_ROPE), 1.0),
        'c_ctx': nrm((D_MODEL,), 1.0),
        'ln1_g': gain((DEPTH, D_MODEL)),
        'ln2_g': gain((DEPTH, D_MODEL)),
        'w_ada': nrm((DEPTH, D_MODEL, 6 * D_MODEL), 0.5 * D_MODEL ** -0.5),
        'b_ada': nrm((DEPTH, 6 * D_MODEL), 0.02),
        'w_in': nrm((DEPTH, D_MODEL, IN_WIDTH), D_MODEL ** -0.5),
        'w_out': nrm((DEPTH, MIX_WIDTH, D_MODEL), MIX_WIDTH ** -0.5),
        'na_rpb': nrm((DEPTH, NA_HEADS, 2 * NA_KH - 1, 2 * NA_KW - 1), 0.1),
        'gdn_conv_w': nrm((DEPTH, GDN_CONV, GDN_QKV), GDN_CONV ** -0.5),
        'gdn_a_log': a_log,
        'gdn_dt_bias': dt + jnp.log(-jnp.expm1(-dt)),
        'gdn_norm_g': gain((DEPTH, GDN_DV)),
        'conv_dw_w': nrm((DEPTH, CONV_WIDTH, CONV_CH), CONV_WIDTH ** -0.5),
        'conv_dw_b': nrm((DEPTH, CONV_CH), 0.02),
        'conv_ln_g': gain((DEPTH, CONV_CH)),
        'conv_ln_b': nrm((DEPTH, CONV_CH), 0.02),
        'mla_q_norm_g': gain((DEPTH, MLA_Q_RANK)),
        'mla_kv_norm_g': gain((DEPTH, MLA_KV_RANK)),
        'mla_w_uq': nrm((DEPTH, MLA_Q_RANK, MLA_HEADS * (MLA_NOPE + MLA_ROPE)), MLA_Q_RANK ** -0.5),
        'mla_w_ukv': nrm((DEPTH, MLA_KV_RANK, MLA_HEADS * (MLA_NOPE + MLA_V)), MLA_KV_RANK ** -0.5),
        'router_w': nrm((DEPTH, D_MODEL, N_EXPERTS), D_MODEL ** -0.5),
        'router_b': nrm((DEPTH, N_EXPERTS), 0.01),
        'exp_w_gu': nrm((DEPTH, N_EXPERTS, D_MODEL, 2 * D_EXPERT), D_MODEL ** -0.5),
        'exp_b_gu': nrm((DEPTH, N_EXPERTS, 2 * D_EXPERT), 0.02),
        'exp_w_down': nrm((DEPTH, N_EXPERTS, D_EXPERT, D_MODEL), D_EXPERT ** -0.5),
        'exp_b_down': nrm((DEPTH, N_EXPERTS, D_MODEL), 0.02),
        'final_norm_g': gain((D_MODEL,)),
    }


def reference(x_prompt, x_sample, c, cache_na_k, cache_na_v, state_gdn, cache_mla_ckv, cache_mla_krope, c_ctx,
              ln1_g, ln2_g, w_ada, b_ada, w_in, w_out, na_rpb, gdn_conv_w, gdn_a_log, gdn_dt_bias, gdn_norm_g,
              conv_dw_w, conv_dw_b, conv_ln_g, conv_ln_b, mla_q_norm_g, mla_kv_norm_g, mla_w_uq, mla_w_ukv,
              router_w, router_b, exp_w_gu, exp_b_gu, exp_w_down, exp_b_down, final_norm_g):
    def layer_params(l):
        return dict(ln1_g=ln1_g[l], ln2_g=ln2_g[l], w_ada=w_ada[l], b_ada=b_ada[l], w_in=w_in[l], w_out=w_out[l],
                    na_rpb=na_rpb[l], gdn_conv_w=gdn_conv_w[l], gdn_a_log=gdn_a_log[l], gdn_dt_bias=gdn_dt_bias[l],
                    gdn_norm_g=gdn_norm_g[l], conv_dw_w=conv_dw_w[l], conv_dw_b=conv_dw_b[l], conv_ln_g=conv_ln_g[l],
                    conv_ln_b=conv_ln_b[l], mla_q_norm_g=mla_q_norm_g[l], mla_kv_norm_g=mla_kv_norm_g[l],
                    mla_w_uq=mla_w_uq[l], mla_w_ukv=mla_w_ukv[l], router_w=router_w[l], router_b=router_b[l],
                    exp_w_gu=exp_w_gu[l], exp_b_gu=exp_b_gu[l], exp_w_down=exp_w_down[l], exp_b_down=exp_b_down[l])

    xp = x_prompt
    na_k_list, na_v_list, gdn_list, ckv_list, kr_list = [], [], [], [], []
    for l in range(DEPTH):
        xp, (ka, va, s_ctx, ckv_n, kr) = trunk_layer(xp, c_ctx[None, :], layer_params(l), None)
        na_k_list.append(ka)
        na_v_list.append(va)
        gdn_list.append(s_ctx)
        ckv_list.append(ckv_n)
        kr_list.append(kr)

    t = jnp.arange(x_sample.shape[1])
    pos_r = (t // GRID_W).astype(jnp.float32)
    pos_c = (t % GRID_W).astype(jnp.float32)
    xs = x_sample
    for l in range(DEPTH):
        lat = (pos_r, pos_c, cache_na_k[:, l], cache_na_v[:, l], state_gdn[:, l], cache_mla_ckv[:, l], cache_mla_krope[:, l])
        xs, _ = trunk_layer(xs, c, layer_params(l), lat)

    y_prompt = rmsnorm(xp, final_norm_g)
    y_sample = rmsnorm(xs, final_norm_g)
    new_na_k = jnp.stack(na_k_list, axis=1)
    new_na_v = jnp.stack(na_v_list, axis=1)
    new_gdn_state = jnp.stack(gdn_list, axis=1)
    new_mla_ckv = jnp.stack(ckv_list, axis=1)
    new_mla_krope = jnp.stack(kr_list, axis=1)
    return (y_prompt, y_sample, new_na_k, new_na_v, new_gdn_state, new_mla_ckv, new_mla_krope)
```

```python
import functools
import math

import jax
import jax.numpy as jnp
import numpy as np
from jax import lax
from jax.experimental import pallas as pl
from jax.experimental.pallas import tpu as pltpu

D_MODEL = 2048
BATCH = 32
SEQ = 256
DEPTH = 4
DEC_BATCH = 8
DEC_SEQ = 1024
PAST_LEN = 256

GRID_W = 64
EPS = 1e-6
NEG_INF = -1e30
Q_BLOCK = 128

MIX_WIDTH = D_MODEL
GROUP_WIDTH = MIX_WIDTH // 4
HEAD_DIM = 128

NA_HEADS = GROUP_WIDTH // HEAD_DIM
NA_KH = 8
NA_KW = 16
NA_KB = 2 * NA_KW
NA_NCB = GRID_W // NA_KW

GDN_HEADS = GROUP_WIDTH // HEAD_DIM
GDN_DK = HEAD_DIM
GDN_DV = HEAD_DIM
GDN_CONV = 4
GDN_CHUNK = 64
GDN_QKV = GDN_HEADS * (2 * GDN_DK + GDN_DV)

CONV_CH = GROUP_WIDTH
CONV_WIDTH = 31

MLA_HEADS = GROUP_WIDTH // HEAD_DIM
MLA_Q_RANK = 3 * D_MODEL // 16
MLA_KV_RANK = D_MODEL // 16
MLA_NOPE = HEAD_DIM
MLA_ROPE = HEAD_DIM // 2
MLA_V = HEAD_DIM
ROPE_BASE = 10000.0

N_EXPERTS = 32
TOP_K = 4
D_EXPERT = D_MODEL
SWIGLU_LIMIT = 7.0
SWIGLU_ALPHA = 1.702

SPLIT_SIZES = (3 * NA_HEADS * HEAD_DIM, GDN_QKV + GDN_HEADS * GDN_DV, 2 * GDN_HEADS, 2 * GDN_HEADS,
               2 * CONV_CH, MLA_Q_RANK, MLA_KV_RANK, MLA_ROPE)
IN_WIDTH = int(sum(SPLIT_SIZES))

N_PROMPT = BATCH * SEQ
N_SAMPLE = DEC_BATCH * DEC_SEQ
N_TOK = N_PROMPT + N_SAMPLE
N_MOD = 1 + DEC_BATCH

LANE = 128
P_A = 0
P_B_QKV = 3 * NA_HEADS * HEAD_DIM
P_B_Z = P_B_QKV + GDN_QKV
P_C = P_B_Z + GDN_HEADS * GDN_DV
P_D_CQ = P_C + 2 * CONV_CH
P_D_CKV = P_D_CQ + MLA_Q_RANK
P_D_KR = P_D_CKV + MLA_KV_RANK
P_B_BETA = P_D_KR + MLA_ROPE
P_B_A = P_B_BETA + 2 * GDN_HEADS
PROJ_TN = 768
P_WIDTH = -(-(P_B_A + 2 * GDN_HEADS) // PROJ_TN) * PROJ_TN

PROJ_TM = 512
OUT_TM = 256

MOE_TM = 1024
MOE_SB = 256
MOE_NSUB = MOE_TM // MOE_SB
MOE_TF = 256
MOE_NF = D_EXPERT // MOE_TF
MOE_ROWS = N_TOK * TOP_K + N_EXPERTS * MOE_TM
MOE_NBLK = MOE_ROWS // MOE_TM

VMEM_LIMIT = 56 * 1024 * 1024

F32 = jnp.float32
BF16 = jnp.bfloat16


def _mod_row(i, tm):
    n_prompt_blocks = N_PROMPT // tm
    per_batch = DEC_SEQ // tm
    return jnp.where(i < n_prompt_blocks, 0, 1 + (i - n_prompt_blocks) // per_batch)


def _mm_kernel(a_ref, w_ref, o_ref):
    o_ref[...] = jnp.dot(a_ref[...].astype(BF16), w_ref[...].astype(BF16), preferred_element_type=F32)


def matmul(a, w, tm, tn):
    m, k = a.shape
    n = w.shape[1]
    assert m % tm == 0 and n % tn == 0
    return pl.pallas_call(
        _mm_kernel,
        grid=(m // tm, n // tn),
        in_specs=[pl.BlockSpec((tm, k), lambda i, j: (i, 0)),
                  pl.BlockSpec((k, tn), lambda i, j: (0, j))],
        out_specs=pl.BlockSpec((tm, tn), lambda i, j: (i, j)),
        out_shape=jax.ShapeDtypeStruct((m, n), F32),
        compiler_params=pltpu.CompilerParams(dimension_semantics=("parallel", "parallel"),
                                             vmem_limit_bytes=VMEM_LIMIT),
        name="matmul",
    )(a, w)


def _proj_in_kernel(x_ref, g_ref, sc_ref, sh_ref, w_ref, o_ref, h_scr):
    @pl.when(pl.program_id(1) == 0)
    def _():
        x = x_ref[...]
        y = x * lax.rsqrt(jnp.mean(x * x, axis=-1, keepdims=True) + EPS) * g_ref[...]
        h_scr[...] = (y * (1.0 + sc_ref[0]) + sh_ref[0]).astype(BF16)

    o_ref[...] = jnp.dot(h_scr[...], w_ref[...], preferred_element_type=F32)


def proj_in(x, ln_g, mod, w_packed):
    tm, tn = PROJ_TM, PROJ_TN
    return pl.pallas_call(
        _proj_in_kernel,
        grid=(N_TOK // tm, P_WIDTH // tn),
        in_specs=[pl.BlockSpec((tm, D_MODEL), lambda i, j: (i, 0)),
                  pl.BlockSpec((1, D_MODEL), lambda i, j: (0, 0)),
                  pl.BlockSpec((1, 1, D_MODEL), lambda i, j: (_mod_row(i, tm) * 6 + 1, 0, 0)),
                  pl.BlockSpec((1, 1, D_MODEL), lambda i, j: (_mod_row(i, tm) * 6 + 0, 0, 0)),
                  pl.BlockSpec((D_MODEL, tn), lambda i, j: (0, j))],
        out_specs=pl.BlockSpec((tm, tn), lambda i, j: (i, j)),
        out_shape=jax.ShapeDtypeStruct((N_TOK, P_WIDTH), F32),
        scratch_shapes=[pltpu.VMEM((tm, D_MODEL), BF16)],
        compiler_params=pltpu.CompilerParams(dimension_semantics=("parallel", "arbitrary"),
                                             vmem_limit_bytes=VMEM_LIMIT),
        name="proj_in",
    )(x, ln_g, mod, mod, w_packed)


def _proj_out_kernel(m_ref, x_ref, w_ref, g1_ref, g_ref, sc_ref, sh_ref, rw_ref, rb_ref,
                     xo_ref, h_ref, lg_ref):
    mix = jnp.dot(m_ref[...].astype(BF16), w_ref[...], preferred_element_type=F32)
    x = x_ref[...] + g1_ref[0] * mix
    xo_ref[...] = x
    y = x * lax.rsqrt(jnp.mean(x * x, axis=-1, keepdims=True) + EPS) * g_ref[...]
    h = y * (1.0 + sc_ref[0]) + sh_ref[0]
    h_ref[...] = h.astype(BF16)
    lg_ref[...] = jnp.dot(h, rw_ref[...], precision=lax.Precision.HIGHEST,
                          preferred_element_type=F32) + rb_ref[...]


def proj_out(mixed, x, w_out_bf16, ln_g, mod, router_w, router_b):
    tm = OUT_TM
    row = lambda i: (i, 0)
    const = lambda i: (0, 0)
    mod_spec = lambda k: pl.BlockSpec((1, 1, D_MODEL), lambda i: (_mod_row(i, tm) * 6 + k, 0, 0))
    return pl.pallas_call(
        _proj_out_kernel,
        grid=(N_TOK // tm,),
        in_specs=[pl.BlockSpec((tm, MIX_WIDTH), row),
                  pl.BlockSpec((tm, D_MODEL), row),
                  pl.BlockSpec((MIX_WIDTH, D_MODEL), const),
                  mod_spec(2),
                  pl.BlockSpec((1, D_MODEL), const),
                  mod_spec(4),
                  mod_spec(3),
                  pl.BlockSpec((D_MODEL, N_EXPERTS), const),
                  pl.BlockSpec((1, N_EXPERTS), const)],
        out_specs=[pl.BlockSpec((tm, D_MODEL), row),
                   pl.BlockSpec((tm, D_MODEL), row),
                   pl.BlockSpec((tm, N_EXPERTS), row)],
        out_shape=[jax.ShapeDtypeStruct((N_TOK, D_MODEL), F32),
                   jax.ShapeDtypeStruct((N_TOK, D_MODEL), BF16),
                   jax.ShapeDtypeStruct((N_TOK, N_EXPERTS), F32)],
        compiler_params=pltpu.CompilerParams(dimension_semantics=("parallel",),
                                             vmem_limit_bytes=VMEM_LIMIT),
        name="proj_out",
    )(mixed, x, w_out_bf16, mod, ln_g, mod, mod, router_w, router_b)


def _moe_kernel(be_ref, bn_ref, nv_ref, x_ref, wg_ref, wl_ref, bg_ref, bl_ref, wd_ref, bd_ref, o_ref,
                wg_s, wl_s, wd_s):
    i = pl.program_id(0)
    j = pl.program_id(1)
    nv = nv_ref[0]

    @pl.when(i < nv)
    def _():
        wg_s[...] = wg_ref[...].astype(BF16)
        wl_s[...] = wl_ref[...].astype(BF16)
        wd_s[...] = wd_ref[...].astype(BF16)
        for s in range(MOE_NSUB):
            rows = pl.ds(s * MOE_SB, MOE_SB)

            @pl.when(s < bn_ref[i])
            def _():
                x = x_ref[rows, :]
                g = jnp.dot(x, wg_s[...], preferred_element_type=F32) + bg_ref[...]
                u = jnp.dot(x, wl_s[...], preferred_element_type=F32) + bl_ref[...]
                g = jnp.minimum(g, SWIGLU_LIMIT)
                u = jnp.clip(u, -SWIGLU_LIMIT, SWIGLU_LIMIT)
                y = g * jax.nn.sigmoid(SWIGLU_ALPHA * g) * (u + 1.0)
                part = jnp.dot(y.astype(BF16), wd_s[...], preferred_element_type=F32)

                @pl.when(j == 0)
                def _():
                    o_ref[rows, :] = part + bd_ref[...]

                @pl.when(j > 0)
                def _():
                    o_ref[rows, :] += part

            @pl.when(jnp.logical_and(s >= bn_ref[i], j == 0))
            def _():
                o_ref[rows, :] = jnp.zeros((MOE_SB, D_MODEL), F32)

    @pl.when(jnp.logical_and(i == nv, j == 0))
    def _():
        o_ref[...] = jnp.zeros((MOE_TM, D_MODEL), F32)


def moe_experts(layer, xs, blk_e, blk_n, n_valid, w_gu, b_gu, w_dn, b_dn):
    tf = MOE_TF

    def blk(i, nv):
        return jnp.minimum(i, nv[0] - 1)

    def ftile(i, j, nv):
        return jnp.where(i < nv[0], j, MOE_NF - 1)

    grid_spec = pltpu.PrefetchScalarGridSpec(
        num_scalar_prefetch=3,
        grid=(MOE_NBLK, MOE_NF),
        in_specs=[
            pl.BlockSpec((MOE_TM, D_MODEL), lambda i, j, be, bn, nv: (blk(i, nv), 0)),
            pl.BlockSpec((None, None, D_MODEL, tf),
                         lambda i, j, be, bn, nv: (layer, be[blk(i, nv)], 0, ftile(i, j, nv))),
            pl.BlockSpec((None, None, D_MODEL, tf),
                         lambda i, j, be, bn, nv: (layer, be[blk(i, nv)], 0, MOE_NF + ftile(i, j, nv))),
            pl.BlockSpec((None, None, 1, tf),
                         lambda i, j, be, bn, nv: (layer, be[blk(i, nv)], 0, ftile(i, j, nv))),
            pl.BlockSpec((None, None, 1, tf),
                         lambda i, j, be, bn, nv: (layer, be[blk(i, nv)], 0, MOE_NF + ftile(i, j, nv))),
            pl.BlockSpec((None, None, tf, D_MODEL),
                         lambda i, j, be, bn, nv: (layer, be[blk(i, nv)], ftile(i, j, nv), 0)),
            pl.BlockSpec((None, None, 1, D_MODEL),
                         lambda i, j, be, bn, nv: (layer, be[blk(i, nv)], 0, 0)),
        ],
        out_specs=pl.BlockSpec((MOE_TM, D_MODEL),
                               lambda i, j, be, bn, nv: (jnp.where(i < nv[0], i, MOE_NBLK - 1), 0)),
        scratch_shapes=[pltpu.VMEM((D_MODEL, tf), BF16), pltpu.VMEM((D_MODEL, tf), BF16),
                        pltpu.VMEM((tf, D_MODEL), BF16)],
    )
    b_gu4 = b_gu.reshape(DEPTH, N_EXPERTS, 1, 2 * D_EXPERT)
    b_dn4 = b_dn.reshape(DEPTH, N_EXPERTS, 1, D_MODEL)
    return pl.pallas_call(
        _moe_kernel,
        grid_spec=grid_spec,
        out_shape=jax.ShapeDtypeStruct((MOE_ROWS, D_MODEL), F32),
        compiler_params=pltpu.CompilerParams(dimension_semantics=("arbitrary", "arbitrary"),
                                             vmem_limit_bytes=VMEM_LIMIT),
        name="moe_experts",
    )(blk_e, blk_n, n_valid, xs, w_gu, w_gu, b_gu4, b_gu4, w_dn, b_dn4)


def moe(layer, h_bf16, logits, w_gu, b_gu, w_dn, b_dn):
    n = N_TOK
    top_v, top_i = lax.top_k(logits, TOP_K)
    gates = jax.nn.softmax(top_v, axis=-1)
    flat_e = top_i.reshape(-1).astype(jnp.int32)
    counts = jnp.bincount(flat_e, length=N_EXPERTS).astype(jnp.int32)
    padded = (counts + MOE_TM - 1) // MOE_TM * MOE_TM
    pad_end = jnp.cumsum(padded)
    pad_start = pad_end - padded
    start = jnp.cumsum(counts) - counts
    order = jnp.argsort(flat_e)
    e_sorted = flat_e[order]
    dest_sorted = pad_start[e_sorted] + jnp.arange(n * TOP_K, dtype=jnp.int32) - start[e_sorted]
    row_tok = jnp.zeros((MOE_ROWS,), jnp.int32).at[dest_sorted].set((order // TOP_K).astype(jnp.int32))
    dest = jnp.zeros((n * TOP_K,), jnp.int32).at[order].set(dest_sorted)
    blk_row0 = jnp.arange(MOE_NBLK, dtype=jnp.int32) * MOE_TM
    blk_e = jnp.minimum(jnp.searchsorted(pad_end, blk_row0, side='right'), N_EXPERTS - 1).astype(jnp.int32)
    left = counts[blk_e] - (blk_row0 - pad_start[blk_e])
    blk_n = jnp.clip((left + MOE_SB - 1) // MOE_SB, 0, MOE_NSUB).astype(jnp.int32)
    n_valid = (pad_end[-1:] // MOE_TM).astype(jnp.int32)

    xs = h_bf16[row_tok]
    out = moe_experts(layer, xs, blk_e, blk_n, n_valid, w_gu, b_gu, w_dn, b_dn)
    picked = out[dest].reshape(n, TOP_K, D_MODEL)
    return jnp.sum(picked * gates[:, :, None], axis=1)


def _rmsnorm_kernel(x_ref, g_ref, o_ref):
    x = x_ref[...]
    o_ref[...] = x * lax.rsqrt(jnp.mean(x * x, axis=-1, keepdims=True) + EPS) * g_ref[...]


def final_norm(x, g):
    tm = 512
    return pl.pallas_call(
        _rmsnorm_kernel,
        grid=(N_TOK // tm,),
        in_specs=[pl.BlockSpec((tm, D_MODEL), lambda i: (i, 0)),
                  pl.BlockSpec((1, D_MODEL), lambda i: (0, 0))],
        out_specs=pl.BlockSpec((tm, D_MODEL), lambda i: (i, 0)),
        out_shape=jax.ShapeDtypeStruct((N_TOK, D_MODEL), F32),
        compiler_params=pltpu.CompilerParams(dimension_semantics=("parallel",)),
        name="final_norm",
    )(x, g.reshape(1, D_MODEL))


def rmsnorm(x, g):
    xf = x.astype(F32)
    y = xf * lax.rsqrt(jnp.mean(xf * xf, axis=-1, keepdims=True) + EPS)
    return (y * g.astype(F32)).astype(x.dtype)


def layernorm(x, g, b):
    xf = x.astype(F32)
    mu = jnp.mean(xf, axis=-1, keepdims=True)
    xc = xf - mu
    y = xc * lax.rsqrt(jnp.mean(xc * xc, axis=-1, keepdims=True) + EPS)
    return (y * g.astype(F32) + b.astype(F32)).astype(x.dtype)


def l2norm(x):
    return x * lax.rsqrt(jnp.sum(x * x, axis=-1, keepdims=True) + EPS)


def depthwise_conv(x, w, pad_l, pad_r):
    return lax.conv_general_dilated(x, w[:, None, :].astype(x.dtype), window_strides=(1,),
                                    padding=[(pad_l, pad_r)], dimension_numbers=('NWC', 'WIO', 'NWC'),
                                    feature_group_count=x.shape[-1])


def axial_rope(x, pos_r, pos_c):
    half = x.shape[-1] // 2
    inv = 1.0 / (ROPE_BASE ** (jnp.arange(0, half, 2, dtype=F32) / half))

    def rot(xa, pos):
        ang = pos[:, None] * inv[None, :]
        ang = jnp.concatenate([ang, ang], axis=-1)
        shape = (pos.shape[0],) + (1,) * (xa.ndim - 3) + (half,)
        cos = jnp.cos(ang).reshape(shape).astype(xa.dtype)
        sin = jnp.sin(ang).reshape(shape).astype(xa.dtype)
        x1, x2 = jnp.split(xa, 2, axis=-1)
        return xa * cos + jnp.concatenate([-x2, x1], axis=-1) * sin

    return jnp.concatenate([rot(x[..., :half], pos_r), rot(x[..., half:], pos_c)], axis=-1)


def dense_attention(q, k, v):
    B, S, H, dq = q.shape
    nb = S // Q_BLOCK
    scale = dq ** -0.5
    qb = jnp.swapaxes(q.reshape(B, nb, Q_BLOCK, H, dq), 0, 1)

    def block(qi):
        s = jnp.einsum('bqhd,bkhd->bhqk', qi, k).astype(F32) * scale
        pr = jax.nn.softmax(s, axis=-1).astype(v.dtype)
        return jnp.einsum('bhqk,bkhd->bqhd', pr, v)

    o = lax.map(block, qb)
    return jnp.swapaxes(o, 0, 1).reshape(B, S, H, v.shape[-1])


def neighbourhood_attention(q, k, v, k_ctx, v_ctx, rpb):
    B, S, H, dh = q.shape
    rows = S // GRID_W
    kh = min(NA_KH, rows)
    scale = dh ** -0.5
    r = np.arange(rows)
    row_start = np.clip(r - kh // 2, 0, rows - kh)
    row_idx = row_start[:, None] + np.arange(kh)
    q_cols = np.arange(GRID_W).reshape(NA_NCB, NA_KW)
    col_start = np.clip(q_cols - NA_KW // 2, 0, GRID_W - NA_KW)
    band_start = np.minimum(col_start[:, 0], GRID_W - NA_KB)
    band_cols = band_start[:, None] + np.arange(NA_KB)
    n_keys = kh * NA_KB
    tok_idx = (row_idx[:, None, :, None] * GRID_W + band_cols[None, :, None, :]).reshape(rows, NA_NCB, n_keys)
    in_win = (band_cols[:, None, :] >= col_start[:, :, None]) & (band_cols[:, None, :] < col_start[:, :, None] + NA_KW)
    mask = np.broadcast_to(in_win[:, :, None, :], (NA_NCB, NA_KW, kh, NA_KB)).reshape(NA_NCB, NA_KW, n_keys)
    dr = row_idx - r[:, None]
    dc = band_cols[:, None, :] - q_cols[:, :, None]
    ridx = (dr + NA_KH - 1)[:, None, None, :, None]
    cidx = np.clip(dc + NA_KW - 1, 0, 2 * NA_KW - 2)[None, :, :, None, :]
    bias = rpb[:, ridx, cidx].reshape(H, rows, NA_NCB, NA_KW, n_keys).astype(F32)
    kg = jnp.take(k, tok_idx, axis=1)
    vg = jnp.take(v, tok_idx, axis=1)
    qb = q.reshape(B, rows, NA_NCB, NA_KW, H, dh)
    s_loc = jnp.einsum('brnqhd,brnkhd->bhrnqk', qb, kg).astype(F32) * scale + bias[None]
    s_loc = jnp.where(mask, s_loc, NEG_INF)
    s_ctx = jnp.einsum('brnqhd,blhd->bhrnql', qb, k_ctx).astype(F32) * scale
    pr = jax.nn.softmax(jnp.concatenate([s_loc, s_ctx], axis=-1), axis=-1).astype(v.dtype)
    o = (jnp.einsum('bhrnqk,brnkhd->brnqhd', pr[..., :n_keys], vg)
         + jnp.einsum('bhrnql,blhd->brnqhd', pr[..., n_keys:], v_ctx))
    return o.reshape(B, S, H, dh)


def chunk_gated_delta(q, k, v, g, beta, s0):
    B, T, H, dk = q.shape
    n = T // GDN_CHUNK

    def chunks(t):
        t = t.reshape((B, n, GDN_CHUNK, H) + t.shape[3:])
        return jnp.moveaxis(t, (1, 3), (0, 2))

    q = chunks(q) * dk ** -0.5
    k = chunks(k)
    v = chunks(v)
    beta = chunks(beta)
    gc = jnp.cumsum(chunks(g), axis=-1)
    idx = jnp.arange(GDN_CHUNK)
    tril = idx[:, None] >= idx[None, :]
    strict = idx[:, None] > idx[None, :]
    decay = jnp.exp(jnp.where(tril, gc[..., :, None] - gc[..., None, :], -jnp.inf))
    kb = k * beta[..., None]
    amat = jnp.where(strict, jnp.einsum('...id,...jd->...ij', kb, k) * decay, 0.0) + jnp.eye(GDN_CHUNK, dtype=F32)
    u = lax.linalg.triangular_solve(amat, v * beta[..., None], left_side=True, lower=True, unit_diagonal=True)
    w = lax.linalg.triangular_solve(amat, kb * jnp.exp(gc)[..., None], left_side=True, lower=True, unit_diagonal=True)
    qk = jnp.where(tril, jnp.einsum('...id,...jd->...ij', q, k) * decay, 0.0)

    def step(s, xs):
        q_i, k_i, u_i, w_i, gc_i, qk_i = xs
        v_new = u_i - jnp.einsum('bhcd,bhde->bhce', w_i, s)
        o = jnp.einsum('bhcd,bhde->bhce', q_i * jnp.exp(gc_i)[..., None], s) + jnp.einsum('bhij,bhje->bhie', qk_i, v_new)
        g_last = gc_i[..., -1]
        s = s * jnp.exp(g_last)[..., None, None] + jnp.einsum(
            'bhcd,bhce->bhde', k_i * jnp.exp(g_last[..., None] - gc_i)[..., None], v_new)
        return s, o

    s_fin, o = lax.scan(step, s0.astype(F32), (q, k, u, w, gc, qk))
    o = jnp.moveaxis(o, (0, 2), (1, 3)).reshape(B, T, H, v.shape[-1])
    return o, s_fin


def gdn_mixer(qkv_raw, z_raw, beta_raw, a_raw, conv_w, a_log, dt_bias, norm_g, s0):
    B, T, _ = qkv_raw.shape
    qkv = jax.nn.silu(depthwise_conv(qkv_raw, conv_w, (GDN_CONV - 1) // 2, GDN_CONV // 2)).astype(F32)
    z = z_raw.reshape(B, T, GDN_HEADS, GDN_DV)
    q, k, v = jnp.split(qkv, (GDN_HEADS * GDN_DK, 2 * GDN_HEADS * GDN_DK), axis=-1)
    q = l2norm(q.reshape(B, T, GDN_HEADS, GDN_DK))
    k = l2norm(k.reshape(B, T, GDN_HEADS, GDN_DK))
    v = v.reshape(B, T, GDN_HEADS, GDN_DV)
    beta = jax.nn.sigmoid(beta_raw.astype(F32)).reshape(B, T, 2, GDN_HEADS)
    g = -jnp.exp(a_log.astype(F32)) * jax.nn.softplus(
        a_raw.astype(F32).reshape(B, T, 2, GDN_HEADS) + dt_bias.astype(F32))

    def flip(t):
        return jnp.flip(t, axis=1)

    o_f, s_f = chunk_gated_delta(q, k, v, g[:, :, 0], beta[:, :, 0], s0[:, 0])
    o_b, s_b = chunk_gated_delta(flip(q), flip(k), flip(v), flip(g[:, :, 1]), flip(beta[:, :, 1]), s0[:, 1])
    o = rmsnorm(o_f + flip(o_b), norm_g) * jax.nn.silu(z.astype(F32))
    return o.reshape(B, T, GDN_HEADS * GDN_DV), jnp.stack([s_f, s_b], axis=1)


def conformer_conv(u, dw_w, dw_b, ln_g, ln_b):
    a, b = jnp.split(u, 2, axis=-1)
    h = a * jax.nn.sigmoid(b)
    h = depthwise_conv(h, dw_w, CONV_WIDTH // 2, CONV_WIDTH // 2) + dw_b
    return jax.nn.silu(layernorm(h, ln_g, ln_b))


def mla_keys(k_nope, k_rope):
    B, M, H, _ = k_nope.shape
    return jnp.concatenate([k_nope, jnp.broadcast_to(k_rope[:, :, None, :], (B, M, H, k_rope.shape[-1]))], axis=-1)


def token_mixers(proj, p, lat):
    B, T, _ = proj.shape
    a_qkv = proj[..., P_A:P_B_QKV].reshape(B, T, 3, NA_HEADS, HEAD_DIM)
    qa, ka, va = a_qkv[:, :, 0], a_qkv[:, :, 1], a_qkv[:, :, 2]
    b_qkv = proj[..., P_B_QKV:P_B_Z]
    b_z = proj[..., P_B_Z:P_C]
    c_in = proj[..., P_C:P_D_CQ]
    d_cq = proj[..., P_D_CQ:P_D_CKV]
    d_ckv = proj[..., P_D_CKV:P_D_KR]
    d_kr = proj[..., P_D_KR:P_B_BETA]
    b_beta = proj[..., P_B_BETA:P_B_A]
    b_a = proj[..., P_B_A:P_B_A + 2 * GDN_HEADS]

    n = B * T
    cq_n = rmsnorm(d_cq, p['mla_q_norm_g']).reshape(n, MLA_Q_RANK)
    qd = matmul(cq_n, p['mla_w_uq'], 512, MLA_HEADS * (MLA_NOPE + MLA_ROPE))
    qd = qd.reshape(B, T, MLA_HEADS, MLA_NOPE + MLA_ROPE)
    ckv_n = rmsnorm(d_ckv, p['mla_kv_norm_g'])
    kv = matmul(ckv_n.reshape(n, MLA_KV_RANK), p['mla_w_ukv'], 512, MLA_HEADS * (MLA_NOPE + MLA_V))
    kv = kv.reshape(B, T, MLA_HEADS, MLA_NOPE + MLA_V)
    k_nope, vd = kv[..., :MLA_NOPE], kv[..., MLA_NOPE:]
    if lat is None:
        oa = dense_attention(qa, ka, va)
        s0 = jnp.zeros((B, 2, GDN_HEADS, GDN_DK, GDN_DV), F32)
        od = dense_attention(qd, mla_keys(k_nope, d_kr), vd)
    else:
        pos_r, pos_c, ctx_k, ctx_v, s0, ctx_ckv, ctx_kr = lat
        oa = neighbourhood_attention(qa, ka, va, ctx_k, ctx_v, p['na_rpb'])
        q_rot = jnp.concatenate([qd[..., :MLA_NOPE], axial_rope(qd[..., MLA_NOPE:], pos_r, pos_c)], axis=-1)
        kvc = matmul(ctx_ckv.reshape(B * PAST_LEN, MLA_KV_RANK), p['mla_w_ukv'], 512,
                     MLA_HEADS * (MLA_NOPE + MLA_V)).reshape(B, PAST_LEN, MLA_HEADS, MLA_NOPE + MLA_V)
        kc_nope, vc = kvc[..., :MLA_NOPE], kvc[..., MLA_NOPE:]
        keys = jnp.concatenate([mla_keys(k_nope, axial_rope(d_kr, pos_r, pos_c)), mla_keys(kc_nope, ctx_kr)], axis=1)
        od = dense_attention(q_rot, keys, jnp.concatenate([vd, vc], axis=1))
    ob, s_fin = gdn_mixer(b_qkv, b_z, b_beta, b_a, p['gdn_conv_w'], p['gdn_a_log'], p['gdn_dt_bias'],
                          p['gdn_norm_g'], s0)
    oc = conformer_conv(c_in, p['conv_dw_w'], p['conv_dw_b'], p['conv_ln_g'], p['conv_ln_b'])
    mixed = jnp.concatenate([oa.reshape(B, T, -1), ob, oc, od.reshape(B, T, -1)], axis=-1)
    ctx_tensors = (ka, va, s_fin, ckv_n, d_kr) if lat is None else None
    return mixed.reshape(n, MIX_WIDTH), ctx_tensors


def kernel(x_prompt, x_sample, c, cache_na_k, cache_na_v, state_gdn, cache_mla_ckv, cache_mla_krope, c_ctx,
           ln1_g, ln2_g, w_ada, b_ada, w_in, w_out, na_rpb, gdn_conv_w, gdn_a_log, gdn_dt_bias, gdn_norm_g,
           conv_dw_w, conv_dw_b, conv_ln_g, conv_ln_b, mla_q_norm_g, mla_kv_norm_g, mla_w_uq, mla_w_ukv,
           router_w, router_b, exp_w_gu, exp_b_gu, exp_w_down, exp_b_down, final_norm_g):
    x = jnp.concatenate([x_prompt.reshape(N_PROMPT, D_MODEL), x_sample.reshape(N_SAMPLE, D_MODEL)], axis=0)

    cvec = jnp.concatenate([c_ctx[None, :], c, jnp.zeros((16 - N_MOD, D_MODEL), F32)], axis=0)
    cact = jax.nn.silu(cvec)

    t = jnp.arange(DEC_SEQ)
    pos_r = (t // GRID_W).astype(F32)
    pos_c = (t % GRID_W).astype(F32)

    na_k_list, na_v_list, gdn_list, ckv_list, kr_list = [], [], [], [], []
    for l in range(DEPTH):
        p = dict(na_rpb=na_rpb[l], gdn_conv_w=gdn_conv_w[l], gdn_a_log=gdn_a_log[l], gdn_dt_bias=gdn_dt_bias[l],
                 gdn_norm_g=gdn_norm_g[l], conv_dw_w=conv_dw_w[l], conv_dw_b=conv_dw_b[l], conv_ln_g=conv_ln_g[l],
                 conv_ln_b=conv_ln_b[l], mla_q_norm_g=mla_q_norm_g[l], mla_kv_norm_g=mla_kv_norm_g[l],
                 mla_w_uq=mla_w_uq[l], mla_w_ukv=mla_w_ukv[l])
        mod = matmul(cact, w_ada[l], 16, 1536)[:N_MOD] + b_ada[l][None, :]
        mod = mod.reshape(N_MOD * 6, 1, D_MODEL)

        wl = w_in[l]
        w_packed = jnp.concatenate(
            [wl[:, :3584], wl[:, 3600:], wl[:, 3584:3600],
             jnp.zeros((D_MODEL, P_WIDTH - IN_WIDTH), F32)], axis=1).astype(BF16)
        proj = proj_in(x, ln1_g[l].reshape(1, D_MODEL), mod, w_packed)

        mixed_p, ctx = token_mixers(proj[:N_PROMPT].reshape(BATCH, SEQ, P_WIDTH), p, None)
        lat = (pos_r, pos_c, cache_na_k[:, l], cache_na_v[:, l], state_gdn[:, l], cache_mla_ckv[:, l],
               cache_mla_krope[:, l])
        mixed_s, _ = token_mixers(proj[N_PROMPT:].reshape(DEC_BATCH, DEC_SEQ, P_WIDTH), p, lat)
        ka, va, s_ctx, ckv_n, kr = ctx
        na_k_list.append(ka)
        na_v_list.append(va)
        gdn_list.append(s_ctx)
        ckv_list.append(ckv_n)
        kr_list.append(kr)

        mixed = jnp.concatenate([mixed_p, mixed_s], axis=0)
        x, h2, logits = proj_out(mixed, x, w_out[l].astype(BF16), ln2_g[l].reshape(1, D_MODEL), mod,
                                 router_w[l], router_b[l].reshape(1, N_EXPERTS))
        y = moe(l, h2, logits, exp_w_gu, exp_b_gu, exp_w_down, exp_b_down)
        g2 = jnp.concatenate([jnp.broadcast_to(mod[5], (N_PROMPT, D_MODEL)),
                              jnp.repeat(mod[6 + 5::6, 0, :], DEC_SEQ, axis=0)], axis=0)
        x = x + g2 * y

    y_all = final_norm(x, final_norm_g)
    y_prompt = y_all[:N_PROMPT].reshape(BATCH, SEQ, D_MODEL)
    y_sample = y_all[N_PROMPT:].reshape(DEC_BATCH, DEC_SEQ, D_MODEL)
    return (y_prompt, y_sample, jnp.stack(na_k_list, axis=1), jnp.stack(na_v_list, axis=1),
            jnp.stack(gdn_list, axis=1), jnp.stack(ckv_list, axis=1), jnp.stack(kr_list, axis=1))
```
